```python
import jax, jax.numpy as jnp
from jax import lax
import numpy as np


D_MODEL = 2048
BATCH = 2
SEQ = 4096
DEPTH = 2
DEC_BATCH = 128
DEC_SEQ = 4
PAST_LEN = 2048
PAGE_SIZE = 128

N_MIXERS = 2
N_ATTN_LAYERS = (DEPTH + 1) // 2
N_GMLP_LAYERS = DEPTH // 2
DIL_WINDOWS = (128, 512, 2048)
DIL_RATES = (1, 4, 16)
N_DIL_GROUPS = len(DIL_WINDOWS)
HEADS_PER_GROUP = 8
HEAD_DIM = 128
ATTN_WIDTH = HEADS_PER_GROUP * HEAD_DIM
QKV_WIDTH = 3 * N_DIL_GROUPS * ATTN_WIDTH
GMLP_WIDTH = D_MODEL
GMLP_GROUPS = 8
GMLP_CHUNK = 128
N_EXPERTS = 32
TOP_K = 4
D_EXPERT = D_MODEL
SWIGLU_ALPHA = 1.702
SWIGLU_LIMIT = 7.0
MOE_BLOCK = 128
DN_ALPHA = (2 * DEPTH) ** 0.25
DN_BETA = (8 * DEPTH) ** -0.25
LN_EPS = 1e-5
NEG_INF = -1e30

kernel_name = 'dilated_swa_gmlp_moe_decode_step'


def _layer_norm(x, g, b):
    xf = x.astype(jnp.float32)
    mu = jnp.mean(xf, axis=-1, keepdims=True)
    var = jnp.mean(jnp.square(xf - mu), axis=-1, keepdims=True)
    return ((xf - mu) * lax.rsqrt(var + LN_EPS) * g + b).astype(x.dtype)


def _dilated_band_attention(q, k, v, window, dilation):
    bsz, seq, heads, hd = q.shape
    nk = window // dilation
    sub_len = seq // dilation
    n_blk = -(-sub_len // nk)
    pad_len = n_blk * nk

    def subsample(a):
        a = a.reshape(bsz, sub_len, dilation, heads, hd).transpose(0, 2, 1, 3, 4)
        return jnp.pad(a, ((0, 0), (0, 0), (0, pad_len - sub_len), (0, 0), (0, 0)))

    def band(a):
        a = jnp.pad(a, ((0, 0), (0, 0), (nk, 0), (0, 0), (0, 0)))
        prev = a[:, :, :pad_len].reshape(bsz, dilation, n_blk, nk, heads, hd)
        cur = a[:, :, nk:].reshape(bsz, dilation, n_blk, nk, heads, hd)
        return jnp.concatenate([prev, cur], axis=3)

    qb = subsample(q).reshape(bsz, dilation, n_blk, nk, heads, hd)
    kb = band(subsample(k))
    vb = band(subsample(v))
    s = jnp.einsum('brnqhe,brnkhe->brnhqk', qb, kb).astype(jnp.float32) * (hd ** -0.5)
    qi = jnp.arange(nk)[:, None]
    ki = jnp.arange(2 * nk)[None, :]
    dist = qi + nk - ki
    band_ok = (dist >= 0) & (dist <= nk)
    key_ok = (jnp.arange(n_blk)[:, None] * nk + ki - nk) >= 0
    mask = band_ok[None] & key_ok[:, None, :]
    s = jnp.where(mask[None, None, :, None], s, NEG_INF)
    m = jnp.max(s, axis=-1, keepdims=True)
    p = jnp.exp(s - m)
    den = jnp.sum(p, axis=-1)
    o = jnp.einsum('brnhqk,brnkhe->brnqhe', p, vb) / jnp.swapaxes(den, -1, -2)[..., None]
    lse = jnp.swapaxes(m[..., 0] + jnp.log(den), -1, -2)

    def unsub(a):
        a = a.reshape((bsz, dilation, pad_len) + a.shape[4:])[:, :, :sub_len]
        a = jnp.swapaxes(a, 1, 2)
        return a.reshape((bsz, seq) + a.shape[3:])

    return unsub(o), unsub(lse)


def _dilated_step_attention(q, k, v, k_buf, v_buf, window, dilation):
    bsz, t_new, heads, hd = q.shape
    wb = k_buf.shape[1]
    nk = window // dilation
    t = jnp.arange(t_new)
    j = jnp.arange(nk + 1)
    idx = wb + t[:, None] - j[None, :] * dilation
    past_ok = (idx >= 0) & (idx < wb)
    idx_c = jnp.clip(idx, 0, wb - 1)
    k_past = k_buf[:, idx_c]
    v_past = v_buf[:, idx_c]
    scale = hd ** -0.5
    s_past = jnp.einsum('bthe,btkhe->bthk', q, k_past).astype(jnp.float32) * scale
    s_past = jnp.where(past_ok[None, :, None, :], s_past, NEG_INF)
    dist = t[:, None] - t[None, :]
    new_ok = (dist >= 0) & (dist <= window) & (dist % dilation == 0)
    s_new = jnp.einsum('bthe,bshe->bths', q, k).astype(jnp.float32) * scale
    s_new = jnp.where(new_ok[None, :, None, :], s_new, NEG_INF)
    s = jnp.concatenate([s_past, s_new], axis=-1)
    m = jnp.max(s, axis=-1, keepdims=True)
    p = jnp.exp(s - m)
    den = jnp.sum(p, axis=-1)
    o = (jnp.einsum('bthk,btkhe->bthe', p[..., :nk + 1], v_past)
         + jnp.einsum('bths,bshe->bthe', p[..., nk + 1:], v)) / den[..., None]
    lse = m[..., 0] + jnp.log(den)
    return o, lse


def _merge_groups(outs, lses):
    w = jax.nn.softmax(jnp.stack(lses, axis=0), axis=0)
    return jnp.sum(w[..., None] * jnp.stack(outs, axis=0), axis=0)


def _attention_prompt(x, w_qkv, w_out):
    bsz, seq, _ = x.shape
    qkv = (x @ w_qkv).reshape(bsz, seq, 3, N_DIL_GROUPS, HEADS_PER_GROUP, HEAD_DIM)
    outs, lses, rows = [], [], []
    for g, (window, dil) in enumerate(zip(DIL_WINDOWS, DIL_RATES)):
        q, k, v = qkv[:, :, 0, g], qkv[:, :, 1, g], qkv[:, :, 2, g]
        o, lse = _dilated_band_attention(q, k, v, window, dil)
        keep = min(window, seq)
        rows.append(jnp.stack([k[:, seq - keep:], v[:, seq - keep:]], axis=2))
        outs.append(o)
        lses.append(lse)
    o = _merge_groups(outs, lses).astype(x.dtype).reshape(bsz, seq, ATTN_WIDTH)
    return o @ w_out, rows


def _attention_sample(x, kv_bufs, w_qkv, w_out):
    bsz, t_new, _ = x.shape
    qkv = (x @ w_qkv).reshape(bsz, t_new, 3, N_DIL_GROUPS, HEADS_PER_GROUP, HEAD_DIM)
    outs, lses, rows = [], [], []
    for g, (window, dil) in enumerate(zip(DIL_WINDOWS, DIL_RATES)):
        q, k, v = qkv[:, :, 0, g], qkv[:, :, 1, g], qkv[:, :, 2, g]
        buf = kv_bufs[g]
        o, lse = _dilated_step_attention(q, k, v, buf[:, :, 0], buf[:, :, 1], window, dil)
        rows.append(jnp.stack([k, v], axis=2))
        outs.append(o)
        lses.append(lse)
    o = _merge_groups(outs, lses).astype(x.dtype).reshape(bsz, t_new, ATTN_WIDTH)
    return o @ w_out, rows


def _gmlp_mixer(x, w_in, b_in, v_ln_gain, v_ln_bias, w_spatial, b_spatial, w_out):
    bsz, seq, _ = x.shape
    z = jax.nn.gelu(x @ w_in + b_in, approximate=False)
    u, v = jnp.split(z, 2, axis=-1)
    v = _layer_norm(v, v_ln_gain, v_ln_bias)
    c = min(GMLP_CHUNK, seq)
    n_chunk = seq // c
    vg = v.reshape(bsz, n_chunk, c, GMLP_GROUPS, GMLP_WIDTH // GMLP_GROUPS)
    w_s = jnp.tril(w_spatial[:, :c, :c])
    mixed = jnp.einsum('gij,bnjgc->bnigc', w_s, vg) + jnp.transpose(b_spatial[:, :c])[None, None, :, :, None]
    y = u * mixed.reshape(bsz, seq, GMLP_WIDTH).astype(u.dtype)
    return y @ w_out, v


def _moe_ffn(x, w_router, b_router, w_gate_up, b_gate_up, w_down, b_down):
    shp = x.shape
    xt = x.reshape(-1, D_MODEL)
    n_tok = xt.shape[0]
    logits = (xt @ w_router).astype(jnp.float32) + b_router
    top_val, top_idx = lax.top_k(logits, TOP_K)
    gates = jax.nn.softmax(top_val, axis=-1)
    n_assign = n_tok * TOP_K
    flat_e = top_idx.reshape(-1)
    order = jnp.argsort(flat_e)
    sorted_e = flat_e[order]
    counts = jnp.bincount(flat_e, length=N_EXPERTS)
    padded = (counts + MOE_BLOCK - 1) // MOE_BLOCK * MOE_BLOCK
    pad_end = jnp.cumsum(padded)
    pad_start = pad_end - padded
    u_start = jnp.cumsum(counts) - counts
    slot = pad_start[sorted_e] + jnp.arange(n_assign) - u_start[sorted_e]
    n_blocks = -(-(n_assign + N_EXPERTS * (MOE_BLOCK - 1)) // MOE_BLOCK)
    n_slots = n_blocks * MOE_BLOCK
    slot_tok = jnp.full((n_slots,), n_tok, jnp.int32).at[slot].set((order // TOP_K).astype(jnp.int32))
    slot_gate = jnp.zeros((n_slots,), jnp.float32).at[slot].set(gates.reshape(-1)[order])
    block_e = jnp.minimum(jnp.searchsorted(pad_end, jnp.arange(n_blocks) * MOE_BLOCK, side='right'), N_EXPERTS - 1)
    x_pad = jnp.concatenate([xt, jnp.zeros((1, D_MODEL), xt.dtype)], axis=0)
    xs = x_pad[slot_tok].reshape(n_blocks, MOE_BLOCK, D_MODEL)

    def expert_block(args):
        xb, e = args
        h = xb @ w_gate_up[e] + b_gate_up[e]
        h_glu, h_lin = jnp.split(h, 2, axis=-1)
        h_glu = jnp.minimum(h_glu, SWIGLU_LIMIT)
        h_lin = jnp.clip(h_lin, -SWIGLU_LIMIT, SWIGLU_LIMIT)
        a = h_glu * jax.nn.sigmoid(SWIGLU_ALPHA * h_glu) * (h_lin + 1.0)
        return a @ w_down[e] + b_down[e]

    ys = lax.map(expert_block, (xs, block_e)).reshape(n_slots, D_MODEL)
    y = jax.ops.segment_sum(ys * slot_gate[:, None].astype(ys.dtype), slot_tok, num_segments=n_tok + 1)[:n_tok]
    return y.reshape(shp)


def setup_inputs(seed: int = 0) -> dict:
    key = jax.random.key(seed)
    ks = jax.random.split(key, 24)

    def nrm(k, shape, scale):
        return jax.random.normal(k, shape, jnp.float32) * scale

    inp = {}
    inp['x_prompt'] = nrm(ks[0], (BATCH, SEQ, D_MODEL), 1.0)
    inp['x_sample'] = nrm(ks[1], (DEC_BATCH, DEC_SEQ, D_MODEL), 1.0)
    for i, w in enumerate(DIL_WINDOWS):
        inp['cache_kv_w%d' % w] = nrm(ks[2 + i], (N_ATTN_LAYERS, DEC_BATCH, min(w, PAST_LEN), 2, HEADS_PER_GROUP, HEAD_DIM), 1.0)
    inp['ln_gain'] = 1.0 + nrm(ks[5], (DEPTH, 2, D_MODEL), 0.02)
    inp['ln_bias'] = nrm(ks[6], (DEPTH, 2, D_MODEL), 0.02)
    qkv_scale = jnp.array([1.0, 1.0, DN_BETA], jnp.float32)[:, None]
    w_qkv = nrm(ks[7], (N_ATTN_LAYERS, D_MODEL, 3, N_DIL_GROUPS * ATTN_WIDTH), D_MODEL ** -0.5) * qkv_scale
    inp['attn_w_qkv'] = w_qkv.reshape(N_ATTN_LAYERS, D_MODEL, QKV_WIDTH)
    inp['attn_w_out'] = nrm(ks[8], (N_ATTN_LAYERS, ATTN_WIDTH, D_MODEL), ATTN_WIDTH ** -0.5 * DN_BETA)
    inp['gmlp_w_in'] = nrm(ks[9], (N_GMLP_LAYERS, D_MODEL, 2 * GMLP_WIDTH), D_MODEL ** -0.5)
    inp['gmlp_b_in'] = nrm(ks[10], (N_GMLP_LAYERS, 2 * GMLP_WIDTH), 0.02)
    inp['gmlp_v_ln_gain'] = 1.0 + nrm(ks[11], (N_GMLP_LAYERS, GMLP_WIDTH), 0.02)
    inp['gmlp_v_ln_bias'] = nrm(ks[12], (N_GMLP_LAYERS, GMLP_WIDTH), 0.02)
    inp['gmlp_w_spatial'] = nrm(ks[13], (N_GMLP_LAYERS, GMLP_GROUPS, GMLP_CHUNK, GMLP_CHUNK), GMLP_CHUNK ** -0.5)
    inp['gmlp_b_spatial'] = 1.0 + nrm(ks[14], (N_GMLP_LAYERS, GMLP_GROUPS, GMLP_CHUNK), 0.02)
    inp['gmlp_w_out'] = nrm(ks[15], (N_GMLP_LAYERS, GMLP_WIDTH, D_MODEL), GMLP_WIDTH ** -0.5 * DN_BETA)
    inp['moe_w_router'] = nrm(ks[16], (DEPTH, D_MODEL, N_EXPERTS), D_MODEL ** -0.5)
    inp['moe_b_router'] = nrm(ks[17], (DEPTH, N_EXPERTS), 0.01)
    inp['moe_w_gate_up'] = nrm(ks[18], (DEPTH, N_EXPERTS, D_MODEL, 2 * D_EXPERT), D_MODEL ** -0.5)
    inp['moe_b_gate_up'] = nrm(ks[19], (DEPTH, N_EXPERTS, 2 * D_EXPERT), 0.02)
    inp['moe_w_down'] = nrm(ks[20], (DEPTH, N_EXPERTS, D_EXPERT, D_MODEL), D_EXPERT ** -0.5 * DN_BETA)
    inp['moe_b_down'] = nrm(ks[21], (DEPTH, N_EXPERTS, D_MODEL), 0.02)
    return inp


def reference(x_prompt, x_sample, cache_kv_w128, cache_kv_w512, cache_kv_w2048, ln_gain, ln_bias,
              attn_w_qkv, attn_w_out, gmlp_w_in, gmlp_b_in, gmlp_v_ln_gain, gmlp_v_ln_bias,
              gmlp_w_spatial, gmlp_b_spatial, gmlp_w_out, moe_w_router, moe_b_router,
              moe_w_gate_up, moe_b_gate_up, moe_w_down, moe_b_down):
    kv_caches = (cache_kv_w128, cache_kv_w512, cache_kv_w2048)
    xp, xs = x_prompt, x_sample
    kv_prompt = [[] for _ in range(N_DIL_GROUPS)]
    kv_sample = [[] for _ in range(N_DIL_GROUPS)]
    v_sample = []
    for layer in range(DEPTH):
        if layer % N_MIXERS == 0:
            a = layer // N_MIXERS
            mp, rows_p = _attention_prompt(xp, attn_w_qkv[a], attn_w_out[a])
            ms, rows_s = _attention_sample(xs, tuple(c[a] for c in kv_caches), attn_w_qkv[a], attn_w_out[a])
            for g in range(N_DIL_GROUPS):
                kv_prompt[g].append(rows_p[g])
                kv_sample[g].append(rows_s[g])
        else:
            b = layer // N_MIXERS
            gp = (gmlp_w_in[b], gmlp_b_in[b], gmlp_v_ln_gain[b], gmlp_v_ln_bias[b],
                  gmlp_w_spatial[b], gmlp_b_spatial[b], gmlp_w_out[b])
            mp, _ = _gmlp_mixer(xp, *gp)
            ms, v_rows = _gmlp_mixer(xs, *gp)
            v_sample.append(v_rows)
        xp = _layer_norm(DN_ALPHA * xp + mp, ln_gain[layer, 0], ln_bias[layer, 0])
        xs = _layer_norm(DN_ALPHA * xs + ms, ln_gain[layer, 0], ln_bias[layer, 0])
        mo = (moe_w_router[layer], moe_b_router[layer], moe_w_gate_up[layer], moe_b_gate_up[layer],
              moe_w_down[layer], moe_b_down[layer])
        xp = _layer_norm(DN_ALPHA * xp + _moe_ffn(xp, *mo), ln_gain[layer, 1], ln_bias[layer, 1])
        xs = _layer_norm(DN_ALPHA * xs + _moe_ffn(xs, *mo), ln_gain[layer, 1], ln_bias[layer, 1])
    kv_w128_prompt = jnp.stack(kv_prompt[0], axis=0)
    kv_w512_prompt = jnp.stack(kv_prompt[1], axis=0)
    kv_w2048_prompt = jnp.stack(kv_prompt[2], axis=0)
    kv_w128_sample = jnp.stack(kv_sample[0], axis=0)
    kv_w512_sample = jnp.stack(kv_sample[1], axis=0)
    kv_w2048_sample = jnp.stack(kv_sample[2], axis=0)
    gmlp_v_sample = jnp.stack(v_sample, axis=0)
    return (xp, xs, kv_w128_prompt, kv_w512_prompt, kv_w2048_prompt,
            kv_w128_sample, kv_w512_sample, kv_w2048_sample, gmlp_v_sample)
```

```python
import functools

import jax
import jax.numpy as jnp
from jax import lax
from jax.experimental import pallas as pl
from jax.experimental.pallas import tpu as pltpu

F32 = jnp.float32
BF16 = jnp.bfloat16

DIL_WINDOWS = (128, 512, 2048)
DIL_RATES = (1, 4, 16)
HEADS = 8
HEAD_DIM = 128
ATTN_WIDTH = HEADS * HEAD_DIM
GMLP_GROUPS = 8
GMLP_CHUNK = 128
N_EXPERTS = 32
TOP_K = 4
SWIGLU_ALPHA = 1.702
SWIGLU_LIMIT = 7.0
DEPTH = 2
DN_ALPHA = (2 * DEPTH) ** 0.25
LN_EPS = 1e-5
NEG_INF = -1e30

VMEM_LIMIT_V7X = 56 * 1024 * 1024
LANES = 128
MOE_SUB = 256
MOE_GROUP_SUBS = 5
MOE_GROUP_ROWS = MOE_SUB * MOE_GROUP_SUBS


def _params(*sem):
    return pltpu.CompilerParams(dimension_semantics=sem, vmem_limit_bytes=VMEM_LIMIT_V7X)


def _layer_norm(y, gain, bias):
    mu = jnp.mean(y, axis=-1, keepdims=True)
    var = jnp.mean(jnp.square(y - mu), axis=-1, keepdims=True)
    return (y - mu) * lax.rsqrt(var + LN_EPS) * gain + bias


def _mm_kernel(*refs, n_k, has_bias, act, has_resid, has_ln):
    it = iter(refs)
    x_ref = next(it)
    w_ref = next(it)
    b_ref = next(it) if has_bias else None
    r_ref = next(it) if has_resid else None
    g_ref = next(it) if has_ln else None
    beta_ref = next(it) if has_ln else None
    o_ref = next(it)
    acc_ref = next(it) if n_k > 1 else None

    part = jnp.dot(x_ref[...].astype(BF16), w_ref[...].astype(BF16), preferred_element_type=F32)

    def epilogue(y):
        if has_bias:
            y = y + b_ref[...]
        if act == "gelu":
            y = 0.5 * y * (1.0 + lax.erf(y * (2.0 ** -0.5)))
        if has_resid:
            y = DN_ALPHA * r_ref[...] + y
        if has_ln:
            y = _layer_norm(y, g_ref[...], beta_ref[...])
        o_ref[...] = y.astype(o_ref.dtype)

    if n_k == 1:
        epilogue(part)
        return

    k = pl.program_id(2)

    @pl.when(k == 0)
    def _():
        acc_ref[...] = part

    @pl.when(k != 0)
    def _():
        acc_ref[...] += part

    @pl.when(k == n_k - 1)
    def _():
        epilogue(acc_ref[...])


def _matmul(x, w, layer, *, col0=0, n_out=None, bias=None, act=None, resid=None, ln=None,
            tm, tn, tk):
    m, kdim = x.shape
    n_out = w.shape[2] - col0 if n_out is None else n_out
    assert m % tm == 0 and n_out % tn == 0 and kdim % tk == 0 and col0 % tn == 0
    n_k = kdim // tk
    cb = col0 // tn
    in_specs = [
        pl.BlockSpec((tm, tk), lambda i, j, k: (i, k)),
        pl.BlockSpec((None, tk, tn), lambda i, j, k: (layer, k, j + cb)),
    ]
    args = [x, w]
    if bias is not None:
        in_specs.append(pl.BlockSpec((None, 1, tn), lambda i, j, k: (layer, 0, j + cb)))
        args.append(bias.reshape(bias.shape[0], 1, bias.shape[1]))
    if resid is not None:
        in_specs.append(pl.BlockSpec((tm, tn), lambda i, j, k: (i, j)))
        args.append(resid)
    if ln is not None:
        assert tn == n_out
        in_specs += [pl.BlockSpec((1, tn), lambda i, j, k: (0, 0))] * 2
        args += [ln[0], ln[1]]
    kern = functools.partial(_mm_kernel, n_k=n_k, has_bias=bias is not None, act=act,
                             has_resid=resid is not None, has_ln=ln is not None)
    return pl.pallas_call(
        kern,
        grid=(m // tm, n_out // tn, n_k),
        in_specs=in_specs,
        out_specs=pl.BlockSpec((tm, tn), lambda i, j, k: (i, j)),
        out_shape=jax.ShapeDtypeStruct((m, n_out), F32),
        scratch_shapes=[pltpu.VMEM((tm, tn), F32)] if n_k > 1 else [],
        compiler_params=_params("parallel", "parallel", "arbitrary"),
        name="dense_proj",
    )(*args)


def _attn_prompt_kernel(q_ref, kp_ref, kc_ref, vp_ref, vc_ref, o_ref, lse_ref, *, nk):
    n = pl.program_id(2)
    qi = lax.broadcasted_iota(jnp.int32, (nk, 2 * nk), 0)
    ki = lax.broadcasted_iota(jnp.int32, (nk, 2 * nk), 1)
    dist = qi + nk - ki
    ok = (dist >= 0) & (dist <= nk) & ((ki >= nk) | (n > 0))
    head_lane = lax.broadcasted_iota(jnp.int32, (nk, HEADS), 1)
    scale = HEAD_DIM ** -0.5
    lse = jnp.zeros((nk, HEADS), F32)
    for h in range(HEADS):
        sl = slice(h * HEAD_DIM, (h + 1) * HEAD_DIM)
        q = q_ref[:, sl].astype(BF16)
        k = jnp.concatenate([kp_ref[:, sl], kc_ref[:, sl]], axis=0).astype(BF16)
        v = jnp.concatenate([vp_ref[:, sl], vc_ref[:, sl]], axis=0).astype(BF16)
        s = lax.dot_general(q, k, (((1,), (1,)), ((), ())), preferred_element_type=F32) * scale
        s = jnp.where(ok, s, NEG_INF)
        m = jnp.max(s, axis=-1, keepdims=True)
        p = jnp.exp(s - m)
        den = jnp.sum(p, axis=-1, keepdims=True)
        o_ref[:, sl] = jnp.dot(p.astype(BF16), v, preferred_element_type=F32) / den
        lse = jnp.where(head_lane == h, m + jnp.log(den), lse)
    lse_ref[0] = lse


def _attn_prompt_group(qkv, bsz, seq, g, window, dil):
    n_rows, width = qkv.shape
    n_groups = len(DIL_WINDOWS)
    nk = window // dil
    sub_len = seq // dil
    assert seq % dil == 0 and sub_len % nk == 0 and n_rows % dil == 0
    n_blk = sub_len // nk
    units = width // ATTN_WIDTH
    qkv_v = qkv.reshape(n_rows // dil, dil * width)

    def spec(slot, prev):
        def imap(b, r, n):
            row = jnp.maximum(n - 1, 0) if prev else n
            return (b * n_blk + row, r * units + slot * n_groups + g)
        return pl.BlockSpec((nk, ATTN_WIDTH), imap)

    o, lse = pl.pallas_call(
        functools.partial(_attn_prompt_kernel, nk=nk),
        grid=(bsz, dil, n_blk),
        in_specs=[spec(0, False), spec(1, True), spec(1, False), spec(2, True), spec(2, False)],
        out_specs=[pl.BlockSpec((nk, ATTN_WIDTH), lambda b, r, n: (b * n_blk + n, r)),
                   pl.BlockSpec((1, nk, HEADS), lambda b, r, n: (r, b * n_blk + n, 0))],
        out_shape=[jax.ShapeDtypeStruct((bsz * sub_len, dil * ATTN_WIDTH), F32),
                   jax.ShapeDtypeStruct((dil, bsz * sub_len, HEADS), F32)],
        compiler_params=_params("parallel", "parallel", "arbitrary"),
        name="attn_prompt",
    )(qkv_v, qkv_v, qkv_v, qkv_v, qkv_v)
    return o.reshape(bsz * seq, ATTN_WIDTH), jnp.transpose(lse, (1, 0, 2)).reshape(bsz * seq, HEADS)


def _merge_kernel(o0_ref, o1_ref, o2_ref, l0_ref, l1_ref, l2_ref, out_ref):
    l0, l1, l2 = l0_ref[...], l1_ref[...], l2_ref[...]
    m = jnp.maximum(jnp.maximum(l0, l1), l2)
    e0, e1, e2 = jnp.exp(l0 - m), jnp.exp(l1 - m), jnp.exp(l2 - m)
    tot = e0 + e1 + e2
    w0, w1, w2 = e0 / tot, e1 / tot, e2 / tot
    for h in range(HEADS):
        sl = slice(h * HEAD_DIM, (h + 1) * HEAD_DIM)
        out_ref[:, sl] = (w0[:, h:h + 1] * o0_ref[:, sl] + w1[:, h:h + 1] * o1_ref[:, sl]
                          + w2[:, h:h + 1] * o2_ref[:, sl])


def _merge_groups(outs, lses, tb):
    n = outs[0].shape[0]
    assert n % tb == 0
    o_spec = pl.BlockSpec((tb, ATTN_WIDTH), lambda i: (i, 0))
    l_spec = pl.BlockSpec((tb, HEADS), lambda i: (i, 0))
    return pl.pallas_call(
        _merge_kernel,
        grid=(n // tb,),
        in_specs=[o_spec] * 3 + [l_spec] * 3,
        out_specs=o_spec,
        out_shape=jax.ShapeDtypeStruct((n, ATTN_WIDTH), F32),
        compiler_params=_params("parallel"),
        name="attn_merge",
    )(*outs, *lses)


def _attn_sample_kernel(qkv_ref, c0_ref, c1_ref, c2_ref, sel_ref, exp_ref, o_ref, *, t_new, n_rows):
    caches = (c0_ref, c1_ref, c2_ref)
    n_groups = len(DIL_WINDOWS)
    scale = HEAD_DIM ** -0.5
    kv_w = 2 * ATTN_WIDTH
    sel = sel_ref[...]
    expand = exp_ref[...]
    row = lax.broadcasted_iota(jnp.int32, (n_rows, LANES), 0)
    new_row = lax.broadcasted_iota(jnp.int32, (8, LANES), 0)

    def block(slot, g):
        c0 = (slot * n_groups + g) * ATTN_WIDTH
        return qkv_ref[0, :, c0:c0 + ATTN_WIDTH]

    for t in range(t_new):
        s_past, s_new, m_g, den_g, lse_g = [], [], [], [], []
        for g, (window, dil) in enumerate(zip(DIL_WINDOWS, DIL_RATES)):
            q_row = block(0, g)[t:t + 1, :]
            col = 0 if dil == 1 else t * kv_w
            k_past = caches[g][0, :, col:col + ATTN_WIDTH]
            sp = jnp.dot((k_past * q_row).astype(BF16), sel, preferred_element_type=F32) * scale
            if dil == 1:
                sp = jnp.where(row >= t, sp, NEG_INF)
            sn = jnp.dot((block(1, g) * q_row).astype(BF16), sel, preferred_element_type=F32) * scale
            new_ok = new_row == t
            for s in range(t):
                if t - s <= window and (t - s) % dil == 0:
                    new_ok = new_ok | (new_row == s)
            sn = jnp.where(new_ok, sn, NEG_INF)
            m = jnp.maximum(jnp.max(sp, axis=0, keepdims=True), jnp.max(sn, axis=0, keepdims=True))
            den = (jnp.sum(jnp.exp(sp - m), axis=0, keepdims=True)
                   + jnp.sum(jnp.exp(sn - m), axis=0, keepdims=True))
            s_past.append(sp)
            s_new.append(sn)
            m_g.append(m)
            den_g.append(den)
            lse_g.append(m + jnp.log(den))
        lse_max = jnp.maximum(jnp.maximum(lse_g[0], lse_g[1]), lse_g[2])
        e_g = [jnp.exp(l - lse_max) for l in lse_g]
        tot = e_g[0] + e_g[1] + e_g[2]
        acc = jnp.zeros((1, ATTN_WIDTH), F32)
        for g, (window, dil) in enumerate(zip(DIL_WINDOWS, DIL_RATES)):
            factor = e_g[g] / tot / den_g[g]
            col = ATTN_WIDTH if dil == 1 else t * kv_w + ATTN_WIDTH
            v_past = caches[g][0, :, col:col + ATTN_WIDTH]
            p = (jnp.exp(s_past[g] - m_g[g]) * factor).astype(BF16)
            acc = acc + jnp.sum(jnp.dot(p, expand, preferred_element_type=F32) * v_past,
                                axis=0, keepdims=True)
            pn = (jnp.exp(s_new[g] - m_g[g]) * factor).astype(BF16)
            acc = acc + jnp.sum(jnp.dot(pn, expand, preferred_element_type=F32) * block(2, g),
                                axis=0, keepdims=True)
        o_ref[0, t:t + 1, :] = acc


def _attn_sample(qkv_s, caches, t_new):
    bsz = qkv_s.shape[0]
    width = qkv_s.shape[2]
    kv_w = 2 * ATTN_WIDTH
    n_rows = DIL_WINDOWS[0]
    qkv_p = jnp.pad(qkv_s, ((0, 0), (0, 8 - t_new), (0, 0)))
    views, specs = [], []
    for cache, window, dil in zip(caches, DIL_WINDOWS, DIL_RATES):
        assert cache.shape[1] == window and window // dil == n_rows and (dil == 1 or t_new <= dil)
        views.append(cache.reshape(bsz, n_rows, dil * kv_w))
        cols = kv_w if dil == 1 else t_new * kv_w
        specs.append(pl.BlockSpec((1, n_rows, cols), lambda b: (b, 0, 0)))
    head_of_lane = jnp.arange(ATTN_WIDTH, dtype=jnp.int32) // HEAD_DIM
    sel = (head_of_lane[:, None] == jnp.arange(LANES, dtype=jnp.int32)[None, :]).astype(BF16)
    return pl.pallas_call(
        functools.partial(_attn_sample_kernel, t_new=t_new, n_rows=n_rows),
        grid=(bsz,),
        in_specs=[pl.BlockSpec((1, 8, width), lambda b: (b, 0, 0))] + specs + [
            pl.BlockSpec((ATTN_WIDTH, LANES), lambda b: (0, 0)),
            pl.BlockSpec((LANES, ATTN_WIDTH), lambda b: (0, 0))],
        out_specs=pl.BlockSpec((1, t_new, ATTN_WIDTH), lambda b: (b, 0, 0)),
        out_shape=jax.ShapeDtypeStruct((bsz, t_new, ATTN_WIDTH), F32),
        compiler_params=_params("parallel"),
        name="attn_sample",
    )(qkv_p, *views, sel, sel.T)


def _spatial_kernel(u_ref, v_ref, w_ref, b_ref, y_ref, *, chunk, group_w):
    r = lax.broadcasted_iota(jnp.int32, (chunk, chunk), 0)
    c = lax.broadcasted_iota(jnp.int32, (chunk, chunk), 1)
    for g in range(GMLP_GROUPS):
        sl = slice(g * group_w, (g + 1) * group_w)
        w = jnp.where(c <= r, w_ref[0, g], 0.0).astype(BF16)
        mixed = jnp.dot(w, v_ref[:, sl].astype(BF16), preferred_element_type=F32) + b_ref[0, :, g:g + 1]
        y_ref[:, sl] = u_ref[:, sl] * mixed


def _spatial_gate(u, v, w_kinds, b_kinds, n_first_kind):
    n, width = u.shape
    chunk = w_kinds.shape[-1]
    assert n % chunk == 0 and width % GMLP_GROUPS == 0
    row_spec = pl.BlockSpec((chunk, width), lambda i: (i, 0))

    def kind(i):
        return jnp.where(i >= n_first_kind, 1, 0)

    return pl.pallas_call(
        functools.partial(_spatial_kernel, chunk=chunk, group_w=width // GMLP_GROUPS),
        grid=(n // chunk,),
        in_specs=[row_spec, row_spec,
                  pl.BlockSpec((1, GMLP_GROUPS, chunk, chunk), lambda i: (kind(i), 0, 0, 0)),
                  pl.BlockSpec((1, chunk, GMLP_GROUPS), lambda i: (kind(i), 0, 0))],
        out_specs=row_spec,
        out_shape=jax.ShapeDtypeStruct((n, width), F32),
        compiler_params=_params("parallel"),
        name="gmlp_spatial",
    )(u, v, w_kinds, b_kinds)


def _split_bf16(a):
    hi = a.astype(BF16)
    return hi, (a - hi.astype(F32)).astype(BF16)


def _router_kernel(x_ref, w_ref, b_ref, idx_ref, gate_ref, rank_ref, cnt_ref, run_ref, *, tb):
    i = pl.program_id(0)

    @pl.when(i == 0)
    def _():
        run_ref[...] = jnp.zeros_like(run_ref)

    xh, xl = _split_bf16(x_ref[...])
    wh, wl = _split_bf16(w_ref[...])
    logits = (jnp.dot(xh, wh, preferred_element_type=F32)
              + (jnp.dot(xh, wl, preferred_element_type=F32) + jnp.dot(xl, wh, preferred_element_type=F32))
              + b_ref[...])
    lane = lax.broadcasted_iota(jnp.int32, (tb, N_EXPERTS), 1)
    work = logits
    vals, idxs, hots = [], [], []
    for _ in range(TOP_K):
        m = jnp.max(work, axis=-1, keepdims=True)
        first = jnp.min(jnp.where(work == m, lane, N_EXPERTS), axis=-1, keepdims=True)
        hot = lane == first
        vals.append(m)
        idxs.append(first)
        hots.append(hot)
        work = jnp.where(hot, -jnp.inf, work)
    exps = [jnp.exp(v - vals[0]) for v in vals]
    tot = exps[0] + exps[1] + exps[2] + exps[3]

    chosen = jnp.zeros((tb, N_EXPERTS), F32)
    for hot in hots:
        chosen = chosen + hot.astype(F32)
    r = lax.broadcasted_iota(jnp.int32, (tb, tb), 0)
    c = lax.broadcasted_iota(jnp.int32, (tb, tb), 1)
    before = jnp.dot((c < r).astype(BF16), chosen.astype(BF16), preferred_element_type=F32)
    base = run_ref[...] + before

    k_lane = lax.broadcasted_iota(jnp.int32, (tb, TOP_K), 1)
    idx_out = jnp.zeros((tb, TOP_K), jnp.int32)
    gate_out = jnp.zeros((tb, TOP_K), F32)
    rank_out = jnp.zeros((tb, TOP_K), jnp.int32)
    for k in range(TOP_K):
        rank_k = jnp.sum(jnp.where(hots[k], base, 0.0), axis=-1, keepdims=True).astype(jnp.int32)
        idx_out = jnp.where(k_lane == k, idxs[k], idx_out)
        gate_out = jnp.where(k_lane == k, exps[k] / tot, gate_out)
        rank_out = jnp.where(k_lane == k, rank_k, rank_out)
    idx_ref[...] = idx_out
    gate_ref[...] = gate_out
    rank_ref[...] = rank_out
    run_ref[...] += jnp.sum(chosen, axis=0, keepdims=True)
    cnt_ref[...] = run_ref[...].astype(jnp.int32)


def _router(x, w_router, b_router, layer, tb):
    n, d = x.shape
    assert n % tb == 0
    k_spec = pl.BlockSpec((tb, TOP_K), lambda i: (i, 0))
    return pl.pallas_call(
        functools.partial(_router_kernel, tb=tb),
        grid=(n // tb,),
        in_specs=[pl.BlockSpec((tb, d), lambda i: (i, 0)),
                  pl.BlockSpec((None, d, N_EXPERTS), lambda i: (layer, 0, 0)),
                  pl.BlockSpec((None, 1, N_EXPERTS), lambda i: (layer, 0, 0))],
        out_specs=[k_spec, k_spec, k_spec, pl.BlockSpec((1, N_EXPERTS), lambda i: (0, 0))],
        out_shape=[jax.ShapeDtypeStruct((n, TOP_K), jnp.int32), jax.ShapeDtypeStruct((n, TOP_K), F32),
                   jax.ShapeDtypeStruct((n, TOP_K), jnp.int32), jax.ShapeDtypeStruct((1, N_EXPERTS), jnp.int32)],
        scratch_shapes=[pltpu.VMEM((1, N_EXPERTS), F32)],
        compiler_params=_params("arbitrary"),
        name="moe_router",
    )(x, w_router, b_router.reshape(b_router.shape[0], 1, N_EXPERTS))


def _row_copy(src_hbm, dst, sem, src_row, dst_row, n_rows=1):
    return pltpu.make_async_copy(src_hbm.at[pl.ds(src_row, n_rows)], dst.at[pl.ds(dst_row, n_rows)], sem)


def _gate_up_kernel(ge_ref, gr_ref, inv_hbm, x_hbm, wg_ref, wl_ref, bg_ref, bl_ref, o_ref,
                    inv_smem, x_f32, x_bf16, wg_bf16, wl_bf16, sem_idx, sem_rows):
    g = pl.program_id(0)
    j = pl.program_id(1)
    n_sub = (gr_ref[g] + MOE_SUB - 1) // MOE_SUB

    @pl.when((j == 0) & (n_sub > 0))
    def _():
        idx_copy = pltpu.make_async_copy(inv_hbm.at[g], inv_smem, sem_idx)
        idx_copy.start()
        idx_copy.wait()

        def issue(r, carry):
            _row_copy(x_hbm, x_f32, sem_rows, inv_smem[r], r).start()
            return carry
        lax.fori_loop(0, n_sub * MOE_SUB, issue, 0)

        def land(s, carry):
            _row_copy(x_hbm, x_f32, sem_rows, 0, 0, MOE_SUB).wait()
            return carry
        lax.fori_loop(0, n_sub, land, 0)

        def narrow(s, carry):
            r0 = pl.multiple_of(s * MOE_SUB, MOE_SUB)
            x_bf16[pl.ds(r0, MOE_SUB), :] = x_f32[pl.ds(r0, MOE_SUB), :].astype(BF16)
            return carry
        lax.fori_loop(0, n_sub, narrow, 0)

    @pl.when(n_sub > 0)
    def _():
        wg_bf16[...] = wg_ref[...].astype(BF16)
        wl_bf16[...] = wl_ref[...].astype(BF16)

    def compute(s, carry):
        r0 = pl.multiple_of(s * MOE_SUB, MOE_SUB)
        xs = x_bf16[pl.ds(r0, MOE_SUB), :]
        h_glu = jnp.dot(xs, wg_bf16[...], preferred_element_type=F32) + bg_ref[...]
        h_lin = jnp.dot(xs, wl_bf16[...], preferred_element_type=F32) + bl_ref[...]
        h_glu = jnp.minimum(h_glu, SWIGLU_LIMIT)
        h_lin = jnp.clip(h_lin, -SWIGLU_LIMIT, SWIGLU_LIMIT)
        a = h_glu * jax.nn.sigmoid(SWIGLU_ALPHA * h_glu) * (h_lin + 1.0)
        o_ref[pl.ds(r0, MOE_SUB), :] = a.astype(o_ref.dtype)
        return carry
    lax.fori_loop(0, n_sub, compute, 0)

    def clear(s, carry):
        r0 = pl.multiple_of(s * MOE_SUB, MOE_SUB)
        o_ref[pl.ds(r0, MOE_SUB), :] = jnp.zeros((MOE_SUB, o_ref.shape[1]), o_ref.dtype)
        return carry
    lax.fori_loop(n_sub, MOE_GROUP_SUBS, clear, 0)


def _down_kernel(ge_ref, gr_ref, h_ref, w_ref, b_ref, o_ref, w_bf16):
    g = pl.program_id(0)
    n_sub = (gr_ref[g] + MOE_SUB - 1) // MOE_SUB

    @pl.when(n_sub > 0)
    def _():
        w_bf16[...] = w_ref[...].astype(BF16)

    def compute(s, carry):
        r0 = pl.multiple_of(s * MOE_SUB, MOE_SUB)
        o_ref[pl.ds(r0, MOE_SUB), :] = (
            jnp.dot(h_ref[pl.ds(r0, MOE_SUB), :], w_bf16[...], preferred_element_type=F32) + b_ref[...])
        return carry
    lax.fori_loop(0, n_sub, compute, 0)

    def clear(s, carry):
        r0 = pl.multiple_of(s * MOE_SUB, MOE_SUB)
        o_ref[pl.ds(r0, MOE_SUB), :] = jnp.zeros((MOE_SUB, o_ref.shape[1]), F32)
        return carry
    lax.fori_loop(n_sub, MOE_GROUP_SUBS, clear, 0)


def _combine_kernel(slot_ref, x_ref, gate_ref, g_ref, beta_ref, ys_hbm, o_ref, buf, sem, *, tb):
    i = pl.program_id(0)

    def issue(n, carry):
        for k in range(TOP_K):
            _row_copy(ys_hbm, buf.at[k], sem, slot_ref[(i * tb + n) * TOP_K + k], n).start()
        return carry
    lax.fori_loop(0, tb, issue, 0)
    for k in range(TOP_K):
        _row_copy(ys_hbm, buf.at[k], sem, 0, 0, tb).wait()
    gates = gate_ref[...]
    y = gates[:, 0:1] * buf[0]
    for k in range(1, TOP_K):
        y = y + gates[:, k:k + 1] * buf[k]
    o_ref[...] = _layer_norm(DN_ALPHA * x_ref[...] + y, g_ref[...], beta_ref[...])


def _moe_layer(x, layer, w_router, b_router, w_gate_up, b_gate_up, w_down, b_down, ln_g, ln_b,
               *, tb_router, tn_up, tn_down, tb_combine):
    n_tok, d = x.shape
    d_exp = w_down.shape[2]
    n_assign = n_tok * TOP_K
    n_groups = N_EXPERTS + n_assign // MOE_GROUP_ROWS
    n_slots = n_groups * MOE_GROUP_ROWS

    idx, gates, rank, counts = _router(x, w_router, b_router, layer, tb_router)

    counts = counts[0]
    groups_e = (counts + MOE_GROUP_ROWS - 1) // MOE_GROUP_ROWS
    group_end = jnp.cumsum(groups_e)
    group_start = group_end - groups_e
    slot = (group_start[idx] + rank // MOE_GROUP_ROWS) * MOE_GROUP_ROWS + rank % MOE_GROUP_ROWS
    slot_flat = slot.reshape(-1).astype(jnp.int32)
    tok_of_assign = (jnp.arange(n_assign, dtype=jnp.int32) // TOP_K)
    inv = jnp.zeros((n_slots,), jnp.int32).at[slot_flat].set(tok_of_assign).reshape(n_groups, MOE_GROUP_ROWS)
    gid = jnp.arange(n_groups, dtype=jnp.int32)
    n_active = group_end[-1]
    ge = jnp.minimum(jnp.searchsorted(group_end, gid, side="right"), N_EXPERTS - 1).astype(jnp.int32)
    gr = jnp.clip(counts[ge] - (gid - group_start[ge]) * MOE_GROUP_ROWS, 0, MOE_GROUP_ROWS)
    gr = jnp.where(gid < n_active, gr, 0).astype(jnp.int32)
    last = jnp.maximum(n_active - 1, 0)
    ge = jnp.where(gid < n_active, ge, ge[last]).astype(jnp.int32)

    nj_up = d_exp // tn_up
    nj_down = d // tn_down

    def jcol(j, gr_ref, g, nj):
        return jnp.where(gr_ref[g] > 0, j, nj - 1)

    h = pl.pallas_call(
        _gate_up_kernel,
        grid_spec=pltpu.PrefetchScalarGridSpec(
            num_scalar_prefetch=2,
            grid=(n_groups, nj_up),
            in_specs=[
                pl.BlockSpec(memory_space=pl.ANY),
                pl.BlockSpec(memory_space=pl.ANY),
                pl.BlockSpec((None, None, d, tn_up), lambda g, j, ge, gr: (layer, ge[g], 0, jcol(j, gr, g, nj_up))),
                pl.BlockSpec((None, None, d, tn_up),
                             lambda g, j, ge, gr: (layer, ge[g], 0, nj_up + jcol(j, gr, g, nj_up))),
                pl.BlockSpec((None, None, 1, tn_up), lambda g, j, ge, gr: (layer, ge[g], 0, jcol(j, gr, g, nj_up))),
                pl.BlockSpec((None, None, 1, tn_up),
                             lambda g, j, ge, gr: (layer, ge[g], 0, nj_up + jcol(j, gr, g, nj_up))),
            ],
            out_specs=pl.BlockSpec((MOE_GROUP_ROWS, tn_up), lambda g, j, ge, gr: (g, j)),
            scratch_shapes=[
                pltpu.SMEM((MOE_GROUP_ROWS,), jnp.int32),
                pltpu.VMEM((MOE_GROUP_ROWS, d), F32),
                pltpu.VMEM((MOE_GROUP_ROWS, d), BF16),
                pltpu.VMEM((d, tn_up), BF16),
                pltpu.VMEM((d, tn_up), BF16),
                pltpu.SemaphoreType.DMA,
                pltpu.SemaphoreType.DMA,
            ]),
        out_shape=jax.ShapeDtypeStruct((n_slots, d_exp), BF16),
        compiler_params=_params("arbitrary", "arbitrary"),
        name="moe_gate_up",
    )(ge, gr, inv, x, w_gate_up, w_gate_up,
      b_gate_up.reshape(b_gate_up.shape[0], N_EXPERTS, 1, 2 * d_exp),
      b_gate_up.reshape(b_gate_up.shape[0], N_EXPERTS, 1, 2 * d_exp))

    ys = pl.pallas_call(
        _down_kernel,
        grid_spec=pltpu.PrefetchScalarGridSpec(
            num_scalar_prefetch=2,
            grid=(n_groups, nj_down),
            in_specs=[
                pl.BlockSpec((MOE_GROUP_ROWS, d_exp), lambda g, j, ge, gr: (g, 0)),
                pl.BlockSpec((None, None, d_exp, tn_down),
                             lambda g, j, ge, gr: (layer, ge[g], 0, jcol(j, gr, g, nj_down))),
                pl.BlockSpec((None, None, 1, tn_down),
                             lambda g, j, ge, gr: (layer, ge[g], 0, jcol(j, gr, g, nj_down))),
            ],
            out_specs=pl.BlockSpec((MOE_GROUP_ROWS, tn_down), lambda g, j, ge, gr: (g, j)),
            scratch_shapes=[pltpu.VMEM((d_exp, tn_down), BF16)]),
        out_shape=jax.ShapeDtypeStruct((n_slots, d), F32),
        compiler_params=_params("arbitrary", "arbitrary"),
        name="moe_down",
    )(ge, gr, h, w_down, b_down.reshape(b_down.shape[0], N_EXPERTS, 1, d))

    assert n_tok % tb_combine == 0
    row_spec = pl.BlockSpec((tb_combine, d), lambda i, s: (i, 0))
    vec_spec = pl.BlockSpec((1, d), lambda i, s: (0, 0))
    return pl.pallas_call(
        functools.partial(_combine_kernel, tb=tb_combine),
        grid_spec=pltpu.PrefetchScalarGridSpec(
            num_scalar_prefetch=1,
            grid=(n_tok // tb_combine,),
            in_specs=[row_spec, pl.BlockSpec((tb_combine, TOP_K), lambda i, s: (i, 0)), vec_spec, vec_spec,
                      pl.BlockSpec(memory_space=pl.ANY)],
            out_specs=row_spec,
            scratch_shapes=[pltpu.VMEM((TOP_K, tb_combine, d), F32), pltpu.SemaphoreType.DMA]),
        out_shape=jax.ShapeDtypeStruct((n_tok, d), F32),
        compiler_params=_params("arbitrary"),
        name="moe_combine",
    )(slot_flat, x, gates, ln_g, ln_b, ys)


def kernel(x_prompt, x_sample, cache_kv_w128, cache_kv_w512, cache_kv_w2048, ln_gain, ln_bias, attn_w_qkv, attn_w_out, gmlp_w_in, gmlp_b_in, gmlp_v_ln_gain, gmlp_v_ln_bias, gmlp_w_spatial, gmlp_b_spatial, gmlp_w_out, moe_w_router, moe_b_router, moe_w_gate_up, moe_b_gate_up, moe_w_down, moe_b_down):
    bsz, seq, d = x_prompt.shape
    dec_b, dec_t, _ = x_sample.shape
    n_p = bsz * seq
    n_s = dec_b * dec_t
    n_tok = n_p + n_s
    n_groups = len(DIL_WINDOWS)
    caches = (cache_kv_w128, cache_kv_w512, cache_kv_w2048)
    tm = n_tok // 8 if n_tok % 64 == 0 else n_tok
    tm_ln = tm // 2
    tk_ln = min(512, d)

    x = jnp.concatenate([x_prompt.reshape(n_p, d), x_sample.reshape(n_s, d)], axis=0)
    kv_prompt, kv_sample, v_sample = [], [], None

    for layer in range(DEPTH):
        ln = lambda which: (ln_gain[layer, which][None], ln_bias[layer, which][None])
        if layer % 2 == 0:
            a = layer // 2
            qkv = _matmul(x, attn_w_qkv, a, tm=tm, tn=min(512, ATTN_WIDTH), tk=d)
            width = qkv.shape[1]
            qkv_p = qkv[:n_p].reshape(bsz, seq, width)
            qkv_s = qkv[n_p:].reshape(dec_b, dec_t, width)
            outs, lses = [], []
            for g, (window, dil) in enumerate(zip(DIL_WINDOWS, DIL_RATES)):
                o_g, lse_g = _attn_prompt_group(qkv, bsz, seq, g, window, dil)
                outs.append(o_g)
                lses.append(lse_g)
                keep = min(window, seq)
                k_cols = slice((n_groups + g) * ATTN_WIDTH, (n_groups + g + 1) * ATTN_WIDTH)
                v_cols = slice((2 * n_groups + g) * ATTN_WIDTH, (2 * n_groups + g + 1) * ATTN_WIDTH)
                kv_prompt.append(jnp.stack([qkv_p[:, seq - keep:, k_cols], qkv_p[:, seq - keep:, v_cols]], axis=2)
                                 .reshape(bsz, keep, 2, HEADS, HEAD_DIM))
                kv_sample.append(jnp.stack([qkv_s[:, :, k_cols], qkv_s[:, :, v_cols]], axis=2)
                                 .reshape(dec_b, dec_t, 2, HEADS, HEAD_DIM))
            o_p = _merge_groups(outs, lses, tb=min(512, n_p))
            o_s = _attn_sample(qkv_s, tuple(c[a] for c in caches), dec_t)
            mixed_in = jnp.concatenate([o_p, o_s.reshape(n_s, ATTN_WIDTH)], axis=0)
            x = _matmul(mixed_in, attn_w_out, a, resid=x, ln=ln(0), tm=tm_ln, tn=d, tk=tk_ln)
        else:
            b = layer // 2
            gw = gmlp_w_in.shape[2] // 2
            u = _matmul(x, gmlp_w_in, b, n_out=gw, bias=gmlp_b_in, act="gelu", tm=tm, tn=min(512, gw), tk=d)
            v = _matmul(x, gmlp_w_in, b, col0=gw, bias=gmlp_b_in, act="gelu",
                        ln=(gmlp_v_ln_gain[b][None], gmlp_v_ln_bias[b][None]), tm=tm_ln, tn=gw, tk=tk_ln)
            v_sample = v[n_p:].reshape(dec_b, dec_t, gw)
            w_s = gmlp_w_spatial[b]
            b_s = gmlp_b_spatial[b]
            reps = GMLP_CHUNK // dec_t
            w_small = jnp.tril(w_s[:, :dec_t, :dec_t])
            w_diag = jax.vmap(lambda m: jnp.kron(jnp.eye(reps, dtype=F32), m))(w_small)
            w_kinds = jnp.stack([w_s, w_diag], axis=0)
            b_kinds = jnp.stack([b_s.T, jnp.tile(b_s[:, :dec_t], (1, reps)).T], axis=0)
            y = _spatial_gate(u, v, w_kinds, b_kinds, n_p // GMLP_CHUNK)
            x = _matmul(y, gmlp_w_out, b, resid=x, ln=ln(0), tm=tm_ln, tn=d, tk=tk_ln)
        g1, b1 = ln(1)
        x = _moe_layer(x, layer, moe_w_router, moe_b_router, moe_w_gate_up, moe_b_gate_up, moe_w_down,
                       moe_b_down, g1, b1, tb_router=256, tn_up=min(256, d), tn_down=min(512, d),
                       tb_combine=256)

    y_prompt = x[:n_p].reshape(bsz, seq, d)
    y_sample = x[n_p:].reshape(dec_b, dec_t, d)
    return (y_prompt, y_sample,
            kv_prompt[0][None], kv_prompt[1][None], kv_prompt[2][None],
            kv_sample[0][None], kv_sample[1][None], kv_sample[2][None],
            v_sample[None])
```

```python
import functools

import jax
import jax.numpy as jnp
from jax import lax
from jax.experimental import pallas as pl
from jax.experimental.pallas import tpu as pltpu

F32 = jnp.float32
BF16 = jnp.bfloat16

DIL_WINDOWS = (128, 512, 2048)
DIL_RATES = (1, 4, 16)
HEADS = 8
HEAD_DIM = 128
ATTN_WIDTH = HEADS * HEAD_DIM
GMLP_GROUPS = 8
GMLP_CHUNK = 128
N_EXPERTS = 32
TOP_K = 4
SWIGLU_ALPHA = 1.702
SWIGLU_LIMIT = 7.0
DEPTH = 2
DN_ALPHA = (2 * DEPTH) ** 0.25
LN_EPS = 1e-5
NEG_INF = -1e30

VMEM_LIMIT_V7X = 56 * 1024 * 1024
LANES = 128
MOE_SUB = 256
MOE_GROUP_SUBS = 5
MOE_GROUP_ROWS = MOE_SUB * MOE_GROUP_SUBS
DMA_UNROLL = 8
RESIDUE_UNROLL = 4


def _params(*sem):
    return pltpu.CompilerParams(dimension_semantics=sem, vmem_limit_bytes=VMEM_LIMIT_V7X)


def _layer_norm(y, gain, bias):
    mu = jnp.mean(y, axis=-1, keepdims=True)
    var = jnp.mean(jnp.square(y - mu), axis=-1, keepdims=True)
    return (y - mu) * lax.rsqrt(var + LN_EPS) * gain + bias


def _mm_kernel(*refs, n_k, has_bias, act, has_resid, has_ln):
    it = iter(refs)
    x_ref = next(it)
    w_ref = next(it)
    b_ref = next(it) if has_bias else None
    r_ref = next(it) if has_resid else None
    g_ref = next(it) if has_ln else None
    beta_ref = next(it) if has_ln else None
    o_ref = next(it)
    acc_ref = next(it) if n_k > 1 else None

    part = jnp.dot(x_ref[...].astype(BF16), w_ref[...].astype(BF16), preferred_element_type=F32)

    def epilogue(y):
        if has_bias:
            y = y + b_ref[...]
        if act == "gelu":
            y = 0.5 * y * (1.0 + lax.erf(y * (2.0 ** -0.5)))
        if has_resid:
            y = DN_ALPHA * r_ref[...] + y
        if has_ln:
            y = _layer_norm(y, g_ref[...], beta_ref[...])
        o_ref[...] = y.astype(o_ref.dtype)

    if n_k == 1:
        epilogue(part)
        return

    k = pl.program_id(2)

    @pl.when(k == 0)
    def _():
        acc_ref[...] = part

    @pl.when(k != 0)
    def _():
        acc_ref[...] += part

    @pl.when(k == n_k - 1)
    def _():
        epilogue(acc_ref[...])


def _matmul(x, w, layer, *, col0=0, n_out=None, bias=None, act=None, resid=None, ln=None,
            tm, tn, tk):
    m, kdim = x.shape
    n_out = w.shape[2] - col0 if n_out is None else n_out
    assert m % tm == 0 and n_out % tn == 0 and kdim % tk == 0 and col0 % tn == 0
    n_k = kdim // tk
    cb = col0 // tn
    in_specs = [
        pl.BlockSpec((tm, tk), lambda i, j, k: (i, k)),
        pl.BlockSpec((None, tk, tn), lambda i, j, k: (layer, k, j + cb)),
    ]
    args = [x, w]
    if bias is not None:
        in_specs.append(pl.BlockSpec((None, 1, tn), lambda i, j, k: (layer, 0, j + cb)))
        args.append(bias.reshape(bias.shape[0], 1, bias.shape[1]))
    if resid is not None:
        in_specs.append(pl.BlockSpec((tm, tn), lambda i, j, k: (i, j)))
        args.append(resid)
    if ln is not None:
        assert tn == n_out
        in_specs += [pl.BlockSpec((1, tn), lambda i, j, k: (0, 0))] * 2
        args += [ln[0], ln[1]]
    kern = functools.partial(_mm_kernel, n_k=n_k, has_bias=bias is not None, act=act,
                             has_resid=resid is not None, has_ln=ln is not None)
    return pl.pallas_call(
        kern,
        grid=(m // tm, n_out // tn, n_k),
        in_specs=in_specs,
        out_specs=pl.BlockSpec((tm, tn), lambda i, j, k: (i, j)),
        out_shape=jax.ShapeDtypeStruct((m, n_out), F32),
        scratch_shapes=[pltpu.VMEM((tm, tn), F32)] if n_k > 1 else [],
        compiler_params=_params("parallel", "parallel", "arbitrary"),
        name="dense_proj",
    )(*args)


def _attn_prompt_kernel(q_ref, kp_ref, k_ref, vp_ref, v_ref, o_ref, lse_ref, *, nk, dil, n_bands, heads):
    first_tile = pl.program_id(2) == 0
    qi = lax.broadcasted_iota(jnp.int32, (nk, 2 * nk), 0)
    ki = lax.broadcasted_iota(jnp.int32, (nk, 2 * nk), 1)
    dist = qi + nk - ki
    band_ok = (dist >= 0) & (dist <= nk)
    lane = lax.broadcasted_iota(jnp.int32, (nk, LANES), 1)
    scale = HEAD_DIM ** -0.5

    def rows(start):
        return pl.ds(start, nk) if dil == 1 else pl.ds(start, nk, stride=dil)

    def residue(r):
        for band in range(n_bands):
            cur = rows(band * nk * dil + r)
            if band == 0:
                ok = band_ok & ((ki >= nk) | jnp.logical_not(first_tile))
                k_prev_ref, v_prev_ref, prev = kp_ref, vp_ref, rows(r)
            else:
                ok = band_ok
                k_prev_ref, v_prev_ref, prev = k_ref, v_ref, rows((band - 1) * nk * dil + r)
            lse_tile = jnp.zeros((nk, LANES), F32)
            for h in range(heads):
                cols = pl.ds(h * HEAD_DIM, HEAD_DIM)
                q = q_ref[cur, cols].astype(BF16)
                k = jnp.concatenate([k_prev_ref[prev, cols], k_ref[cur, cols]], axis=0).astype(BF16)
                v = jnp.concatenate([v_prev_ref[prev, cols], v_ref[cur, cols]], axis=0).astype(BF16)
                s = lax.dot_general(q, k, (((1,), (1,)), ((), ())), preferred_element_type=F32) * scale
                s = jnp.where(ok, s, NEG_INF)
                m = jnp.max(s, axis=-1, keepdims=True)
                p = jnp.exp(s - m)
                den = jnp.sum(p, axis=-1, keepdims=True)
                o_ref[cur, cols] = jnp.dot(p.astype(BF16), v, preferred_element_type=F32) / den
                lse_tile = jnp.where(lane == h, m + jnp.log(den), lse_tile)
            lse_ref[cur, :] = lse_tile

    if dil == 1:
        residue(0)
    else:
        assert dil % RESIDUE_UNROLL == 0

        def trip(c, carry):
            for u in range(RESIDUE_UNROLL):
                residue(c * RESIDUE_UNROLL + u)
            return carry
        lax.fori_loop(0, dil // RESIDUE_UNROLL, trip, 0)


def _attn_prompt_group(qkv, bsz, seq, g, window, dil, *, n_bands, heads):
    n_groups = len(DIL_WINDOWS)
    nk = window // dil
    band_rows = nk * dil
    tile_rows = band_rows * n_bands
    assert seq % tile_rows == 0 and HEADS % heads == 0
    tiles = seq // tile_rows
    col_blocks = HEADS // heads
    width = heads * HEAD_DIM

    def tile_spec(slot):
        return pl.BlockSpec((tile_rows, width),
                            lambda b, c, n: (b * tiles + n, (slot * n_groups + g) * col_blocks + c))

    def prev_spec(slot):
        return pl.BlockSpec((band_rows, width),
                            lambda b, c, n: (jnp.maximum((b * tiles + n) * n_bands - 1, 0),
                                             (slot * n_groups + g) * col_blocks + c))

    return pl.pallas_call(
        functools.partial(_attn_prompt_kernel, nk=nk, dil=dil, n_bands=n_bands, heads=heads),
        grid=(bsz, col_blocks, tiles),
        in_specs=[tile_spec(0), prev_spec(1), tile_spec(1), prev_spec(2), tile_spec(2)],
        out_specs=[pl.BlockSpec((tile_rows, width), lambda b, c, n: (b * tiles + n, c)),
                   pl.BlockSpec((tile_rows, LANES), lambda b, c, n: (b * tiles + n, c))],
        out_shape=[jax.ShapeDtypeStruct((bsz * seq, ATTN_WIDTH), F32),
                   jax.ShapeDtypeStruct((bsz * seq, col_blocks * LANES), F32)],
        compiler_params=_params("parallel", "parallel", "arbitrary"),
        name="attn_prompt",
    )(qkv, qkv, qkv, qkv, qkv)


def _merge_kernel(o0_ref, o1_ref, o2_ref, l0_ref, l1_ref, l2_ref, out_ref, *, heads_per_block):
    o_refs = (o0_ref, o1_ref, o2_ref)
    l_refs = (l0_ref, l1_ref, l2_ref)
    for h in range(HEADS):
        sl = slice(h * HEAD_DIM, (h + 1) * HEAD_DIM)
        lse = []
        for l_ref, hpb in zip(l_refs, heads_per_block):
            lane = (h // hpb) * LANES + h % hpb
            lse.append(l_ref[:, lane:lane + 1])
        m = jnp.maximum(jnp.maximum(lse[0], lse[1]), lse[2])
        e = [jnp.exp(l - m) for l in lse]
        tot = e[0] + e[1] + e[2]
        out_ref[:, sl] = ((e[0] / tot) * o_refs[0][:, sl] + (e[1] / tot) * o_refs[1][:, sl]
                          + (e[2] / tot) * o_refs[2][:, sl])


def _merge_groups(outs, lses, heads_per_block, tb):
    n = outs[0].shape[0]
    assert n % tb == 0
    o_spec = pl.BlockSpec((tb, ATTN_WIDTH), lambda i: (i, 0))
    l_specs = [pl.BlockSpec((tb, l.shape[1]), lambda i: (i, 0)) for l in lses]
    return pl.pallas_call(
        functools.partial(_merge_kernel, heads_per_block=heads_per_block),
        grid=(n // tb,),
        in_specs=[o_spec] * 3 + l_specs,
        out_specs=o_spec,
        out_shape=jax.ShapeDtypeStruct((n, ATTN_WIDTH), F32),
        compiler_params=_params("parallel"),
        name="attn_merge",
    )(*outs, *lses)


def _attn_sample_kernel(qkv_ref, c0_ref, c1_ref, c2_ref, ones_ref, o_ref, *, t_new, n_rows):
    caches = (c0_ref, c1_ref, c2_ref)
    scale = HEAD_DIM ** -0.5
    ones = ones_ref[...]
    key_idx = lax.broadcasted_iota(jnp.int32, (n_rows, HEADS, HEAD_DIM), 0)

    def head_sums(prod):
        flat = prod.reshape(-1, HEAD_DIM).astype(BF16)
        return jnp.dot(flat, ones, preferred_element_type=F32).reshape(prod.shape)

    for t in range(t_new):
        acc_g, den_g, lse_g = [], [], []
        for g, (window, dil) in enumerate(zip(DIL_WINDOWS, DIL_RATES)):
            q = qkv_ref[0, t, 0, g] * scale
            k_past = caches[g][0, :, 0] if dil == 1 else caches[g][0, :, t, 0]
            v_past = caches[g][0, :, 1] if dil == 1 else caches[g][0, :, t, 1]
            sp = head_sums(k_past * q[None])
            if dil == 1:
                sp = jnp.where(key_idx >= t, sp, NEG_INF)
            sn = head_sums(qkv_ref[0, :, 1, g] * q[None])
            new_keys = [s for s in range(t + 1) if t - s <= window and (t - s) % dil == 0]
            m = jnp.max(sp, axis=0)
            for s in new_keys:
                m = jnp.maximum(m, sn[s])
            p = jnp.exp(sp - m[None])
            den = jnp.sum(p, axis=0)
            acc = jnp.sum(p * v_past, axis=0)
            for s in new_keys:
                pn = jnp.exp(sn[s] - m)
                den = den + pn
                acc = acc + pn * qkv_ref[0, s, 2, g]
            acc_g.append(acc)
            den_g.append(den)
            lse_g.append(m + jnp.log(den))
        lse_max = jnp.maximum(jnp.maximum(lse_g[0], lse_g[1]), lse_g[2])
        e_g = [jnp.exp(l - lse_max) for l in lse_g]
        tot = e_g[0] + e_g[1] + e_g[2]
        out = jnp.zeros((HEADS, HEAD_DIM), F32)
        for g in range(len(DIL_WINDOWS)):
            out = out + (e_g[g] / tot / den_g[g]) * acc_g[g]
        o_ref[0, t] = out


def _attn_sample(qkv_s, caches, layer, dec_b, t_new):
    n_groups = len(DIL_WINDOWS)
    n_rows = DIL_WINDOWS[0]
    qkv_t = qkv_s.reshape(dec_b, t_new, 3, n_groups, HEADS, HEAD_DIM)
    views, specs = [], []
    for cache, window, dil in zip(caches, DIL_WINDOWS, DIL_RATES):
        assert cache.shape[2] == window and window // dil == n_rows and (dil == 1 or t_new <= dil)
        n_layers = cache.shape[0]
        if dil == 1:
            views.append(cache)
            specs.append(pl.BlockSpec((None, 1, n_rows, 2, HEADS, HEAD_DIM),
                                      lambda b: (layer, b, 0, 0, 0, 0)))
        else:
            views.append(cache.reshape(n_layers, dec_b, n_rows, dil, 2, HEADS, HEAD_DIM))
            specs.append(pl.BlockSpec((None, 1, n_rows, t_new, 2, HEADS, HEAD_DIM),
                                      lambda b: (layer, b, 0, 0, 0, 0, 0)))
    out = pl.pallas_call(
        functools.partial(_attn_sample_kernel, t_new=t_new, n_rows=n_rows),
        grid=(dec_b,),
        in_specs=[pl.BlockSpec((1, t_new, 3, n_groups, HEADS, HEAD_DIM), lambda b: (b, 0, 0, 0, 0, 0))]
        + specs + [pl.BlockSpec((HEAD_DIM, HEAD_DIM), lambda b: (0, 0))],
        out_specs=pl.BlockSpec((1, t_new, HEADS, HEAD_DIM), lambda b: (b, 0, 0, 0)),
        out_shape=jax.ShapeDtypeStruct((dec_b, t_new, HEADS, HEAD_DIM), F32),
        compiler_params=_params("parallel"),
        name="attn_sample",
    )(qkv_t, *views, jnp.ones((HEAD_DIM, HEAD_DIM), BF16))
    return out.reshape(dec_b * t_new, ATTN_WIDTH)


def _spatial_kernel(u_ref, v_ref, w_ref, b_ref, y_ref, *, chunk, group_w):
    r = lax.broadcasted_iota(jnp.int32, (chunk, chunk), 0)
    c = lax.broadcasted_iota(jnp.int32, (chunk, chunk), 1)
    for g in range(GMLP_GROUPS):
        sl = slice(g * group_w, (g + 1) * group_w)
        w = jnp.where(c <= r, w_ref[0, g], 0.0).astype(BF16)
        mixed = jnp.dot(w, v_ref[:, sl].astype(BF16), preferred_element_type=F32) + b_ref[0, :, g:g + 1]
        y_ref[:, sl] = u_ref[:, sl] * mixed


def _spatial_gate(u, v, w_kinds, b_kinds, n_first_kind):
    n, width = u.shape
    chunk = w_kinds.shape[-1]
    assert n % chunk == 0 and width % GMLP_GROUPS == 0
    row_spec = pl.BlockSpec((chunk, width), lambda i: (i, 0))

    def kind(i):
        return jnp.where(i >= n_first_kind, 1, 0)

    return pl.pallas_call(
        functools.partial(_spatial_kernel, chunk=chunk, group_w=width // GMLP_GROUPS),
        grid=(n // chunk,),
        in_specs=[row_spec, row_spec,
                  pl.BlockSpec((1, GMLP_GROUPS, chunk, chunk), lambda i: (kind(i), 0, 0, 0)),
                  pl.BlockSpec((1, chunk, GMLP_GROUPS), lambda i: (kind(i), 0, 0))],
        out_specs=row_spec,
        out_shape=jax.ShapeDtypeStruct((n, width), F32),
        compiler_params=_params("parallel"),
        name="gmlp_spatial",
    )(u, v, w_kinds, b_kinds)


def _split_bf16(a):
    hi = a.astype(BF16)
    return hi, (a - hi.astype(F32)).astype(BF16)


def _router_kernel(x_ref, w_ref, b_ref, idx_ref, gate_ref, rank_ref, cnt_ref, run_ref, *, tb):
    i = pl.program_id(0)

    @pl.when(i == 0)
    def _():
        run_ref[...] = jnp.zeros_like(run_ref)

    xh, xl = _split_bf16(x_ref[...])
    wh, wl = _split_bf16(w_ref[...])
    logits = (jnp.dot(xh, wh, preferred_element_type=F32)
              + (jnp.dot(xh, wl, preferred_element_type=F32) + jnp.dot(xl, wh, preferred_element_type=F32))
              + b_ref[...])
    lane = lax.broadcasted_iota(jnp.int32, (tb, N_EXPERTS), 1)
    work = logits
    vals, idxs, hots = [], [], []
    for _ in range(TOP_K):
        m = jnp.max(work, axis=-1, keepdims=True)
        first = jnp.min(jnp.where(work == m, lane, N_EXPERTS), axis=-1, keepdims=True)
        hot = lane == first
        vals.append(m)
        idxs.append(first)
        hots.append(hot)
        work = jnp.where(hot, -jnp.inf, work)
    exps = [jnp.exp(v - vals[0]) for v in vals]
    tot = exps[0] + exps[1] + exps[2] + exps[3]

    chosen = jnp.zeros((tb, N_EXPERTS), F32)
    for hot in hots:
        chosen = chosen + hot.astype(F32)
    r = lax.broadcasted_iota(jnp.int32, (tb, tb), 0)
    c = lax.broadcasted_iota(jnp.int32, (tb, tb), 1)
    before = jnp.dot((c < r).astype(BF16), chosen.astype(BF16), preferred_element_type=F32)
    base = run_ref[...] + before

    k_lane = lax.broadcasted_iota(jnp.int32, (tb, TOP_K), 1)
    idx_out = jnp.zeros((tb, TOP_K), jnp.int32)
    gate_out = jnp.zeros((tb, TOP_K), F32)
    rank_out = jnp.zeros((tb, TOP_K), jnp.int32)
    for k in range(TOP_K):
        rank_k = jnp.sum(jnp.where(hots[k], base, 0.0), axis=-1, keepdims=True).astype(jnp.int32)
        idx_out = jnp.where(k_lane == k, idxs[k], idx_out)
        gate_out = jnp.where(k_lane == k, exps[k] / tot, gate_out)
        rank_out = jnp.where(k_lane == k, rank_k, rank_out)
    idx_ref[...] = idx_out
    gate_ref[...] = gate_out
    rank_ref[...] = rank_out
    run_ref[...] += jnp.sum(chosen, axis=0, keepdims=True)
    cnt_ref[...] = run_ref[...].astype(jnp.int32)


def _router(x, w_router, b_router, layer, tb):
    n, d = x.shape
    assert n % tb == 0
    k_spec = pl.BlockSpec((tb, TOP_K), lambda i: (i, 0))
    return pl.pallas_call(
        functools.partial(_router_kernel, tb=tb),
        grid=(n // tb,),
        in_specs=[pl.BlockSpec((tb, d), lambda i: (i, 0)),
                  pl.BlockSpec((None, d, N_EXPERTS), lambda i: (layer, 0, 0)),
                  pl.BlockSpec((None, 1, N_EXPERTS), lambda i: (layer, 0, 0))],
        out_specs=[k_spec, k_spec, k_spec, pl.BlockSpec((1, N_EXPERTS), lambda i: (0, 0))],
        out_shape=[jax.ShapeDtypeStruct((n, TOP_K), jnp.int32), jax.ShapeDtypeStruct((n, TOP_K), F32),
                   jax.ShapeDtypeStruct((n, TOP_K), jnp.int32), jax.ShapeDtypeStruct((1, N_EXPERTS), jnp.int32)],
        scratch_shapes=[pltpu.VMEM((1, N_EXPERTS), F32)],
        compiler_params=_params("arbitrary"),
        name="moe_router",
    )(x, w_router, b_router.reshape(b_router.shape[0], 1, N_EXPERTS))


def _row_copy(src_hbm, dst, sem, src_row, dst_row, n_rows=1):
    return pltpu.make_async_copy(src_hbm.at[pl.ds(src_row, n_rows)], dst.at[pl.ds(dst_row, n_rows)], sem)


def _n_sub(rows):
    return (rows + MOE_SUB - 1) // MOE_SUB


def _for_each_sub(n_sub, fn):
    def pair(p, carry):
        r0 = pl.multiple_of(p * (2 * MOE_SUB), 2 * MOE_SUB)
        fn(r0)
        fn(r0 + MOE_SUB)
        return carry
    lax.fori_loop(0, n_sub // 2, pair, 0)

    @pl.when(n_sub % 2 == 1)
    def _():
        fn(pl.multiple_of((n_sub - 1) * MOE_SUB, MOE_SUB))


def _clear_idle_subs(o_ref, n_sub):
    def clear(s, carry):
        r0 = pl.multiple_of(s * MOE_SUB, MOE_SUB)
        o_ref[pl.ds(r0, MOE_SUB), :] = jnp.zeros((MOE_SUB, o_ref.shape[1]), o_ref.dtype)
        return carry
    lax.fori_loop(n_sub, MOE_GROUP_SUBS, clear, 0)


def _gate_up_kernel(ge_ref, gr_ref, inv_hbm, x_hbm, wg_ref, wl_ref, bg_ref, bl_ref, o_ref,
                    inv_smem, x_f32, x_bf16, wg_bf16, wl_bf16, sem_idx, sem_rows):
    g = pl.program_id(0)
    j = pl.program_id(1)
    n_groups = pl.num_programs(0)
    last_j = pl.num_programs(1) - 1
    n_sub = _n_sub(gr_ref[g])

    def idx_copy(group):
        slot = group % 2
        return pltpu.make_async_copy(inv_hbm.at[group], inv_smem.at[slot], sem_idx.at[slot])

    def issue_rows(group):
        slot = group % 2

        def issue(c, carry):
            r0 = c * DMA_UNROLL
            for u in range(DMA_UNROLL):
                _row_copy(x_hbm, x_f32, sem_rows, inv_smem[slot, r0 + u], r0 + u).start()
            return carry
        lax.fori_loop(0, _n_sub(gr_ref[group]) * (MOE_SUB // DMA_UNROLL), issue, 0)

    @pl.when((g == 0) & (j == 0))
    def _():
        idx_copy(0).start()
        idx_copy(0).wait()
        issue_rows(0)

    @pl.when(j == 0)
    def _():
        def land(s, carry):
            _row_copy(x_hbm, x_f32, sem_rows, 0, 0, MOE_SUB).wait()
            return carry
        lax.fori_loop(0, n_sub, land, 0)

        def narrow(s, carry):
            r0 = pl.multiple_of(s * MOE_SUB, MOE_SUB)
            x_bf16[pl.ds(r0, MOE_SUB), :] = x_f32[pl.ds(r0, MOE_SUB), :].astype(BF16)
            return carry
        lax.fori_loop(0, n_sub, narrow, 0)

        @pl.when(g + 1 < n_groups)
        def _():
            idx_copy(g + 1).start()

    @pl.when((j == last_j) & (g + 1 < n_groups))
    def _():
        idx_copy(g + 1).wait()
        issue_rows(g + 1)

    @pl.when(n_sub > 0)
    def _():
        wg_bf16[...] = wg_ref[...].astype(BF16)
        wl_bf16[...] = wl_ref[...].astype(BF16)

    def compute(r0):
        xs = x_bf16[pl.ds(r0, MOE_SUB), :]
        h_glu = jnp.dot(xs, wg_bf16[...], preferred_element_type=F32) + bg_ref[...]
        h_lin = jnp.dot(xs, wl_bf16[...], preferred_element_type=F32) + bl_ref[...]
        h_glu = jnp.minimum(h_glu, SWIGLU_LIMIT)
        h_lin = jnp.clip(h_lin, -SWIGLU_LIMIT, SWIGLU_LIMIT)
        a = h_glu * jax.nn.sigmoid(SWIGLU_ALPHA * h_glu) * (h_lin + 1.0)
        o_ref[pl.ds(r0, MOE_SUB), :] = a.astype(o_ref.dtype)

    _for_each_sub(n_sub, compute)
    _clear_idle_subs(o_ref, n_sub)


def _down_kernel(ge_ref, gr_ref, h_ref, w_ref, b_ref, o_ref, w_bf16):
    g = pl.program_id(0)
    n_sub = _n_sub(gr_ref[g])

    @pl.when(n_sub > 0)
    def _():
        w_bf16[...] = w_ref[...].astype(BF16)

    def compute(r0):
        o_ref[pl.ds(r0, MOE_SUB), :] = (
            jnp.dot(h_ref[pl.ds(r0, MOE_SUB), :], w_bf16[...], preferred_element_type=F32) + b_ref[...])

    _for_each_sub(n_sub, compute)
    _clear_idle_subs(o_ref, n_sub)


def _combine_kernel(slot_ref, x_ref, gate_ref, g_ref, beta_ref, ys_hbm, o_ref, buf, sem, *, tb):
    i = pl.program_id(0)

    def issue_tile(tile):
        def issue(n, carry):
            for k in range(TOP_K):
                _row_copy(ys_hbm, buf.at[tile % 2, k], sem.at[tile % 2],
                          slot_ref[(tile * tb + n) * TOP_K + k], n).start()
            return carry
        lax.fori_loop(0, tb, issue, 0)

    @pl.when(i == 0)
    def _():
        issue_tile(0)

    @pl.when(i + 1 < pl.num_programs(0))
    def _():
        issue_tile(i + 1)

    cur = i % 2
    for k in range(TOP_K):
        _row_copy(ys_hbm, buf.at[cur, k], sem.at[cur], 0, 0, tb).wait()
    gates = gate_ref[...]
    y = gates[:, 0:1] * buf[cur, 0]
    for k in range(1, TOP_K):
        y = y + gates[:, k:k + 1] * buf[cur, k]
    o_ref[...] = _layer_norm(DN_ALPHA * x_ref[...] + y, g_ref[...], beta_ref[...])


def _moe_layer(x, layer, w_router, b_router, w_gate_up, b_gate_up, w_down, b_down, ln_g, ln_b,
               *, tb_router, tn_up, tn_down, tb_combine):
    n_tok, d = x.shape
    d_exp = w_down.shape[2]
    n_assign = n_tok * TOP_K
    n_groups = N_EXPERTS + n_assign // MOE_GROUP_ROWS
    n_slots = n_groups * MOE_GROUP_ROWS

    idx, gates, rank, counts = _router(x, w_router, b_router, layer, tb_router)

    counts = counts[0]
    groups_e = (counts + MOE_GROUP_ROWS - 1) // MOE_GROUP_ROWS
    group_end = jnp.cumsum(groups_e)
    group_start = group_end - groups_e
    slot = (group_start[idx] + rank // MOE_GROUP_ROWS) * MOE_GROUP_ROWS + rank % MOE_GROUP_ROWS
    slot_flat = slot.reshape(-1).astype(jnp.int32)
    tok_of_assign = (jnp.arange(n_assign, dtype=jnp.int32) // TOP_K)
    inv = jnp.zeros((n_slots,), jnp.int32).at[slot_flat].set(tok_of_assign).reshape(n_groups, MOE_GROUP_ROWS)
    gid = jnp.arange(n_groups, dtype=jnp.int32)
    n_active = group_end[-1]
    ge = jnp.minimum(jnp.searchsorted(group_end, gid, side="right"), N_EXPERTS - 1).astype(jnp.int32)
    gr = jnp.clip(counts[ge] - (gid - group_start[ge]) * MOE_GROUP_ROWS, 0, MOE_GROUP_ROWS)
    gr = jnp.where(gid < n_active, gr, 0).astype(jnp.int32)
    last = jnp.maximum(n_active - 1, 0)
    ge = jnp.where(gid < n_active, ge, ge[last]).astype(jnp.int32)

    nj_up = d_exp // tn_up
    nj_down = d // tn_down

    def jcol(j, gr_ref, g, nj):
        return jnp.where(gr_ref[g] > 0, j, nj - 1)

    h = pl.pallas_call(
        _gate_up_kernel,
        grid_spec=pltpu.PrefetchScalarGridSpec(
            num_scalar_prefetch=2,
            grid=(n_groups, nj_up),
            in_specs=[
                pl.BlockSpec(memory_space=pl.ANY),
                pl.BlockSpec(memory_space=pl.ANY),
                pl.BlockSpec((None, None, d, tn_up), lambda g, j, ge, gr: (layer, ge[g], 0, jcol(j, gr, g, nj_up))),
                pl.BlockSpec((None, None, d, tn_up),
                             lambda g, j, ge, gr: (layer, ge[g], 0, nj_up + jcol(j, gr, g, nj_up))),
                pl.BlockSpec((None, None, 1, tn_up), lambda g, j, ge, gr: (layer, ge[g], 0, jcol(j, gr, g, nj_up))),
                pl.BlockSpec((None, None, 1, tn_up),
                             lambda g, j, ge, gr: (layer, ge[g], 0, nj_up + jcol(j, gr, g, nj_up))),
            ],
            out_specs=pl.BlockSpec((MOE_GROUP_ROWS, tn_up), lambda g, j, ge, gr: (g, j)),
            scratch_shapes=[
                pltpu.SMEM((2, MOE_GROUP_ROWS), jnp.int32),
                pltpu.VMEM((MOE_GROUP_ROWS, d), F32),
                pltpu.VMEM((MOE_GROUP_ROWS, d), BF16),
                pltpu.VMEM((d, tn_up), BF16),
                pltpu.VMEM((d, tn_up), BF16),
                pltpu.SemaphoreType.DMA((2,)),
                pltpu.SemaphoreType.DMA,
            ]),
        out_shape=jax.ShapeDtypeStruct((n_slots, d_exp), BF16),
        compiler_params=_params("arbitrary", "arbitrary"),
        name="moe_gate_up",
    )(ge, gr, inv, x, w_gate_up, w_gate_up,
      b_gate_up.reshape(b_gate_up.shape[0], N_EXPERTS, 1, 2 * d_exp),
      b_gate_up.reshape(b_gate_up.shape[0], N_EXPERTS, 1, 2 * d_exp))

    ys = pl.pallas_call(
        _down_kernel,
        grid_spec=pltpu.PrefetchScalarGridSpec(
            num_scalar_prefetch=2,
            grid=(n_groups, nj_down),
            in_specs=[
                pl.BlockSpec((MOE_GROUP_ROWS, d_exp), lambda g, j, ge, gr: (g, 0)),
                pl.BlockSpec((None, None, d_exp, tn_down),
                             lambda g, j, ge, gr: (layer, ge[g], 0, jcol(j, gr, g, nj_down))),
                pl.BlockSpec((None, None, 1, tn_down),
                             lambda g, j, ge, gr: (layer, ge[g], 0, jcol(j, gr, g, nj_down))),
            ],
            out_specs=pl.BlockSpec((MOE_GROUP_ROWS, tn_down), lambda g, j, ge, gr: (g, j)),
            scratch_shapes=[pltpu.VMEM((d_exp, tn_down), BF16)]),
        out_shape=jax.ShapeDtypeStruct((n_slots, d), F32),
        compiler_params=_params("arbitrary", "arbitrary"),
        name="moe_down",
    )(ge, gr, h, w_down, b_down.reshape(b_down.shape[0], N_EXPERTS, 1, d))

    assert n_tok % tb_combine == 0
    row_spec = pl.BlockSpec((tb_combine, d), lambda i, s: (i, 0))
    vec_spec = pl.BlockSpec((1, d), lambda i, s: (0, 0))
    return pl.pallas_call(
        functools.partial(_combine_kernel, tb=tb_combine),
        grid_spec=pltpu.PrefetchScalarGridSpec(
            num_scalar_prefetch=1,
            grid=(n_tok // tb_combine,),
            in_specs=[row_spec, pl.BlockSpec((tb_combine, TOP_K), lambda i, s: (i, 0)), vec_spec, vec_spec,
                      pl.BlockSpec(memory_space=pl.ANY)],
            out_specs=row_spec,
            scratch_shapes=[pltpu.VMEM((2, TOP_K, tb_combine, d), F32), pltpu.SemaphoreType.DMA((2,))]),
        out_shape=jax.ShapeDtypeStruct((n_tok, d), F32),
        compiler_params=_params("arbitrary"),
        name="moe_combine",
    )(slot_flat, x, gates, ln_g, ln_b, ys)


PROMPT_TILING = ((4, 8), (1, 1), (1, 1))


def kernel(x_prompt, x_sample, cache_kv_w128, cache_kv_w512, cache_kv_w2048, ln_gain, ln_bias, attn_w_qkv, attn_w_out, gmlp_w_in, gmlp_b_in, gmlp_v_ln_gain, gmlp_v_ln_bias, gmlp_w_spatial, gmlp_b_spatial, gmlp_w_out, moe_w_router, moe_b_router, moe_w_gate_up, moe_b_gate_up, moe_w_down, moe_b_down):
    bsz, seq, d = x_prompt.shape
    dec_b, dec_t, _ = x_sample.shape
    n_p = bsz * seq
    n_s = dec_b * dec_t
    n_tok = n_p + n_s
    n_groups = len(DIL_WINDOWS)
    caches = (cache_kv_w128, cache_kv_w512, cache_kv_w2048)
    tm = n_tok // 8 if n_tok % 64 == 0 else n_tok
    tm_ln = tm // 2
    tk_ln = min(512, d)

    x = jnp.concatenate([x_prompt.reshape(n_p, d), x_sample.reshape(n_s, d)], axis=0)
    kv_prompt, kv_sample, v_sample = [], [], None

    for layer in range(DEPTH):
        ln = lambda which: (ln_gain[layer, which][None], ln_bias[layer, which][None])
        if layer % 2 == 0:
            a = layer // 2
            qkv = _matmul(x, attn_w_qkv, a, tm=tm, tn=min(512, ATTN_WIDTH), tk=d)
            width = qkv.shape[1]
            qkv_p = qkv[:n_p].reshape(bsz, seq, width)
            qkv_s = qkv[n_p:]
            outs, lses = [], []
            for g, (window, dil) in enumerate(zip(DIL_WINDOWS, DIL_RATES)):
                n_bands, heads = PROMPT_TILING[g]
                o_g, lse_g = _attn_prompt_group(qkv, bsz, seq, g, window, dil, n_bands=n_bands, heads=heads)
                outs.append(o_g)
                lses.append(lse_g)
                keep = min(window, seq)
                k_cols = slice((n_groups + g) * ATTN_WIDTH, (n_groups + g + 1) * ATTN_WIDTH)
                v_cols = slice((2 * n_groups + g) * ATTN_WIDTH, (2 * n_groups + g + 1) * ATTN_WIDTH)
                kv_prompt.append(jnp.stack([qkv_p[:, seq - keep:, k_cols], qkv_p[:, seq - keep:, v_cols]], axis=2)
                                 .reshape(bsz, keep, 2, HEADS, HEAD_DIM))
                kv_sample.append(jnp.stack([qkv_s[:, k_cols], qkv_s[:, v_cols]], axis=1)
                                 .reshape(dec_b, dec_t, 2, HEADS, HEAD_DIM))
            o_p = _merge_groups(outs, lses, tuple(t[1] for t in PROMPT_TILING), tb=min(512, n_p))
            o_s = _attn_sample(qkv_s, caches, a, dec_b, dec_t)
            mixed_in = jnp.concatenate([o_p, o_s], axis=0)
            x = _matmul(mixed_in, attn_w_out, a, resid=x, ln=ln(0), tm=tm_ln, tn=d, tk=tk_ln)
        else:
            b = layer // 2
            gw = gmlp_w_in.shape[2] // 2
            u = _matmul(x, gmlp_w_in, b, n_out=gw, bias=gmlp_b_in, act="gelu", tm=tm, tn=min(512, gw), tk=d)
            v = _matmul(x, gmlp_w_in, b, col0=gw, bias=gmlp_b_in, act="gelu",
                        ln=(gmlp_v_ln_gain[b][None], gmlp_v_ln_bias[b][None]), tm=tm_ln, tn=gw, tk=tk_ln)
            v_sample = v[n_p:].reshape(dec_b, dec_t, gw)
            w_s = gmlp_w_spatial[b]
            b_s = gmlp_b_spatial[b]
            reps = GMLP_CHUNK // dec_t
            w_small = jnp.tril(w_s[:, :dec_t, :dec_t])
            w_diag = jax.vmap(lambda m: jnp.kron(jnp.eye(reps, dtype=F32), m))(w_small)
            w_kinds = jnp.stack([w_s, w_diag], axis=0)
            b_kinds = jnp.stack([b_s.T, jnp.tile(b_s[:, :dec_t], (1, reps)).T], axis=0)
            y = _spatial_gate(u, v, w_kinds, b_kinds, n_p // GMLP_CHUNK)
            x = _matmul(y, gmlp_w_out, b, resid=x, ln=ln(0), tm=tm_ln, tn=d, tk=tk_ln)
        g1, b1 = ln(1)
        x = _moe_layer(x, layer, moe_w_router, moe_b_router, moe_w_gate_up, moe_b_gate_up, moe_w_down,
                       moe_b_down, g1, b1, tb_router=256, tn_up=min(512, d), tn_down=min(1024, d),
                       tb_combine=256)

    y_prompt = x[:n_p].reshape(bsz, seq, d)
    y_sample = x[n_p:].reshape(dec_b, dec_t, d)
    return (y_prompt, y_sample,
            kv_prompt[0][None], kv_prompt[1][None], kv_prompt[2][None],
            kv_sample[0][None], kv_sample[1][None], kv_sample[2][None],
            v_sample[None])
```
